```python
import jax, jax.numpy as jnp
from jax import lax
import numpy as np

D_MODEL = 1024
BATCH = 2
SEQ = 16384
DEPTH = 4

N_MIXERS = 3
N_CONV_LAYERS = (DEPTH + 2) // 3
N_SGU_LAYERS = (DEPTH + 1) // 3
N_POOL_LAYERS = DEPTH // 3

CONV_WIDTH = 31
CHUNK = 128
SGU_HEADS = 8
SGU_HEAD_DIM = D_MODEL // SGU_HEADS
POOL_WINDOWS = (2, 4, 8, 16)
POOL_GROUPS = len(POOL_WINDOWS)
POOL_GROUP_DIM = D_MODEL // POOL_GROUPS
D_FF = 2816
FFN_CONV_WIDTH = 3
DEEPNORM_ALPHA = float((2 * DEPTH) ** 0.25)
DEEPNORM_BETA = float((8 * DEPTH) ** -0.25)
LN_EPS = 1e-5

kernel_name = "interleaved_conv_sgu_pool_deepnorm_trunk"


def layer_norm(x, g, b):
    xf = x.astype(jnp.float32)
    mu = jnp.mean(xf, axis=-1, keepdims=True)
    var = jnp.mean(jnp.square(xf - mu), axis=-1, keepdims=True)
    y = (xf - mu) * lax.rsqrt(var + LN_EPS) * g.astype(jnp.float32) + b.astype(jnp.float32)
    return y.astype(x.dtype)


def causal_depthwise_conv(x, w):
    k, c = w.shape
    return lax.conv_general_dilated(
        x, w[:, None, :].astype(x.dtype), window_strides=(1,), padding=[(k - 1, 0)],
        dimension_numbers=("NWC", "WIO", "NWC"), feature_group_count=c)


def conformer_conv(x, w_in, dw, dw_b, ln_g, ln_b, w_out):
    h = x @ w_in
    a, gate = jnp.split(h, 2, axis=-1)
    h = a * jax.nn.sigmoid(gate)
    h = causal_depthwise_conv(h, dw) + dw_b
    h = jax.nn.silu(layer_norm(h, ln_g, ln_b))
    return h @ w_out


def chunked_sgu(x, w_in, ln_g, ln_b, ws, bs, w_out):
    bsz, seq, _ = x.shape
    z = jax.nn.gelu(x @ w_in, approximate=False)
    u, v = jnp.split(z, 2, axis=-1)
    v = layer_norm(v, ln_g, ln_b)
    v = v.reshape(bsz, seq // CHUNK, CHUNK, SGU_HEADS, SGU_HEAD_DIM)
    mask = jnp.tril(jnp.ones((CHUNK, CHUNK), dtype=ws.dtype))
    s = jnp.einsum("hts,bnshc->bnthc", ws * mask, v)
    s = s + jnp.transpose(bs)[None, None, :, :, None]
    s = s.reshape(bsz, seq, D_MODEL)
    return (u * s) @ w_out


def multiscale_pool(x, w_in, w_grp, scale, w_out):
    bsz, seq, _ = x.shape
    y = x @ w_in
    yf = y.astype(jnp.float32)
    cs = jnp.concatenate([jnp.zeros((bsz, 1, D_MODEL), jnp.float32),
                          lax.cumsum(yf, axis=1)], axis=1)
    pos = jnp.arange(seq)
    groups = []
    for g, w in enumerate(POOL_WINDOWS):
        sl = slice(g * POOL_GROUP_DIM, (g + 1) * POOL_GROUP_DIM)
        c = cs[..., sl]
        upper = c[:, 1:]
        lower = jnp.pad(c[:, :seq - w + 1], ((0, 0), (w - 1, 0), (0, 0)))
        count = jnp.minimum(pos + 1, w).astype(jnp.float32)[None, :, None]
        groups.append((upper - lower) / count - yf[..., sl])
    p = jnp.stack(groups, axis=2).astype(y.dtype)
    z = jnp.einsum("bsgc,gcd->bsgd", p, w_grp).reshape(bsz, seq, D_MODEL) * scale
    return z @ w_out


def conv_ffn(x, w_up, dw, w_down):
    h = causal_depthwise_conv(x @ w_up, dw)
    g, v = jnp.split(h, 2, axis=-1)
    return (jax.nn.silu(g) * v) @ w_down


def setup_inputs(seed: int = 0) -> dict:
    key = jax.random.key(seed)
    ks = iter(jax.random.split(key, 32))

    def nrm(shape, scale):
        return jax.random.normal(next(ks), shape, jnp.float32) * scale

    d = D_MODEL
    return {
        "x": nrm((BATCH, SEQ, d), 1.0),
        "a_w_in": nrm((N_CONV_LAYERS, d, 2 * d), d ** -0.5),
        "a_dw": nrm((N_CONV_LAYERS, CONV_WIDTH, d), CONV_WIDTH ** -0.5),
        "a_dw_b": nrm((N_CONV_LAYERS, d), 0.02),
        "a_ln_g": 1.0 + nrm((N_CONV_LAYERS, d), 0.02),
        "a_ln_b": nrm((N_CONV_LAYERS, d), 0.02),
        "a_w_out": nrm((N_CONV_LAYERS, d, d), d ** -0.5 * DEEPNORM_BETA),
        "b_w_in": nrm((N_SGU_LAYERS, d, 2 * d), d ** -0.5),
        "b_ln_g": 1.0 + nrm((N_SGU_LAYERS, d), 0.02),
        "b_ln_b": nrm((N_SGU_LAYERS, d), 0.02),
        "b_ws": nrm((N_SGU_LAYERS, SGU_HEADS, CHUNK, CHUNK), CHUNK ** -0.5),
        "b_bs": 1.0 + nrm((N_SGU_LAYERS, SGU_HEADS, CHUNK), 0.01),
        "b_w_out": nrm((N_SGU_LAYERS, d, d), d ** -0.5 * DEEPNORM_BETA),
        "c_w_in": nrm((N_POOL_LAYERS, d, d), d ** -0.5),
        "c_w_grp": nrm((N_POOL_LAYERS, POOL_GROUPS, POOL_GROUP_DIM, POOL_GROUP_DIM), POOL_GROUP_DIM ** -0.5),
        "c_scale": 1.0 + nrm((N_POOL_LAYERS, d), 0.1),
        "c_w_out": nrm((N_POOL_LAYERS, d, d), d ** -0.5 * DEEPNORM_BETA),
        "f_w_up": nrm((DEPTH, d, 2 * D_FF), d ** -0.5),
        "f_dw": nrm((DEPTH, FFN_CONV_WIDTH, 2 * D_FF), FFN_CONV_WIDTH ** -0.5),
        "f_w_down": nrm((DEPTH, D_FF, d), D_FF ** -0.5 * DEEPNORM_BETA),
        "ln1_g": 1.0 + nrm((DEPTH, d), 0.02),
        "ln1_b": nrm((DEPTH, d), 0.02),
        "ln2_g": 1.0 + nrm((DEPTH, d), 0.02),
        "ln2_b": nrm((DEPTH, d), 0.02),
    }


def reference(x, a_w_in, a_dw, a_dw_b, a_ln_g, a_ln_b, a_w_out,
              b_w_in, b_ln_g, b_ln_b, b_ws, b_bs, b_w_out,
              c_w_in, c_w_grp, c_scale, c_w_out,
              f_w_up, f_dw, f_w_down,
              ln1_g, ln1_b, ln2_g, ln2_b):
    for i in range(DEPTH):
        kind, j = i % N_MIXERS, i // N_MIXERS
        if kind == 0:
            h = conformer_conv(x, a_w_in[j], a_dw[j], a_dw_b[j], a_ln_g[j], a_ln_b[j], a_w_out[j])
        elif kind == 1:
            h = chunked_sgu(x, b_w_in[j], b_ln_g[j], b_ln_b[j], b_ws[j], b_bs[j], b_w_out[j])
        else:
            h = multiscale_pool(x, c_w_in[j], c_w_grp[j], c_scale[j], c_w_out[j])
        x = layer_norm(DEEPNORM_ALPHA * x + h, ln1_g[i], ln1_b[i])
        x = layer_norm(DEEPNORM_ALPHA * x + conv_ffn(x, f_w_up[i], f_dw[i], f_w_down[i]), ln2_g[i], ln2_b[i])
    return x
```

```python
import functools

import jax
import jax.numpy as jnp
from jax import lax
from jax.experimental import pallas as pl
from jax.experimental.pallas import tpu as pltpu

SUBLANES = 8
LANES = 128
MXU_DIM = 256
VMEM_LIMIT_BYTES = 56 * 1024 * 1024

LN_EPS = 1e-5
CONV_WIDTH = 31
CONV_HALO = 32
FFN_CONV_WIDTH = 3
SGU_CHUNK = 128
POOL_WINDOWS = (2, 4, 8, 16)
POOL_HALO = 16

ROW_TILE = 512
FF_CHUNK = MXU_DIM

F32 = jnp.float32
BF16 = jnp.bfloat16


def _layer_norm(y, g, b):
    mu = jnp.mean(y, axis=-1, keepdims=True)
    d = y - mu
    var = jnp.mean(d * d, axis=-1, keepdims=True)
    return d * lax.rsqrt(var + LN_EPS) * g + b


def _silu(x):
    return x * jax.nn.sigmoid(x)


def _gelu_exact(x):
    return 0.5 * x * (1.0 + lax.erf(x * (2.0 ** -0.5)))


def _shift_rows(e, s):
    return e if s == 0 else pltpu.roll(e, s, 0)


def _mm(a, b):
    return jnp.dot(a, b, preferred_element_type=F32)


def _ffn_kernel(x_ref, wup_ref, dw_ref, wdn_ref, g_ref, b_ref, o_ref, carry_ref, *,
                alpha, tiles_per_seq, n_chunks):
    tm = x_ref.shape[0]

    @pl.when(pl.program_id(0) % tiles_per_seq == 0)
    def _():
        carry_ref[...] = jnp.zeros_like(carry_ref)

    x = x_ref[...]
    xb = x.astype(BF16)

    def conv_half(j):
        h = _mm(xb, wup_ref[j])
        ext = jnp.concatenate([carry_ref[j], h], axis=0)
        carry_ref[j] = h[tm - SUBLANES:, :]
        w = dw_ref[j]
        out = h * w[2:3, :]
        out = out + _shift_rows(ext, 1)[SUBLANES:, :] * w[1:2, :]
        out = out + _shift_rows(ext, 2)[SUBLANES:, :] * w[0:1, :]
        return out

    acc = jnp.zeros((tm, o_ref.shape[1]), F32)
    for c in range(n_chunks):
        gate = conv_half(c)
        val = conv_half(n_chunks + c)
        act = (_silu(gate) * val).astype(BF16)
        acc = acc + _mm(act, wdn_ref[c])
    y = alpha * x + acc
    o_ref[...] = _layer_norm(y, g_ref[...], b_ref[...])


def _const_spec(shape):
    nd = len(shape)
    return pl.BlockSpec(shape, lambda i: (0,) * nd, pipeline_mode=pl.Buffered(1))


def _row_spec(tm, d):
    return pl.BlockSpec((tm, d), lambda i: (i, 0))


def _compiler_params():
    return pltpu.CompilerParams(dimension_semantics=("arbitrary",),
                                vmem_limit_bytes=VMEM_LIMIT_BYTES)


def _ffn_layer(x2, w_up, dw, w_down, g, b, *, alpha, seq):
    rows, d = x2.shape
    d_ff = w_down.shape[0]
    tm, fc = ROW_TILE, FF_CHUNK
    n_chunks = d_ff // fc
    assert n_chunks * fc == d_ff and seq % tm == 0 and rows % tm == 0
    wup = w_up.astype(BF16).reshape(d, 2 * n_chunks, fc).transpose(1, 0, 2)
    dwc = dw.reshape(FFN_CONV_WIDTH, 2 * n_chunks, fc).transpose(1, 0, 2)
    wdn = w_down.astype(BF16).reshape(n_chunks, fc, d)
    kern = functools.partial(_ffn_kernel, alpha=alpha, tiles_per_seq=seq // tm, n_chunks=n_chunks)
    return pl.pallas_call(
        kern,
        out_shape=jax.ShapeDtypeStruct((rows, d), F32),
        grid=(rows // tm,),
        in_specs=[_row_spec(tm, d), _const_spec(wup.shape), _const_spec(dwc.shape),
                  _const_spec(wdn.shape), _const_spec((1, d)), _const_spec((1, d))],
        out_specs=_row_spec(tm, d),
        scratch_shapes=[pltpu.VMEM((2 * n_chunks, SUBLANES, fc), F32)],
        compiler_params=_compiler_params(),
        name="ffn_layer",
    )(x2, wup, dwc, wdn, g.reshape(1, d), b.reshape(1, d))


def _conf_kernel(x_ref, win_ref, dw_ref, dwb_ref, lng_ref, lnb_ref, wout_ref, g_ref, b_ref,
                 o_ref, u_ref, c_ref, *, alpha, tiles_per_seq):
    tm, d = x_ref.shape

    @pl.when(pl.program_id(0) % tiles_per_seq == 0)
    def _():
        u_ref[0:CONV_HALO, :] = jnp.zeros((CONV_HALO, d), F32)

    x = x_ref[...]
    xb = x.astype(BF16)
    a = _mm(xb, win_ref[0])
    gate = _mm(xb, win_ref[1])
    u_ref[CONV_HALO:, :] = a * jax.nn.sigmoid(gate)

    for cb in range(d // LANES):
        cols = slice(cb * LANES, (cb + 1) * LANES)
        ext = u_ref[:, cols]
        w = dw_ref[:, cols]
        acc = None
        for r in range(SUBLANES):
            rolled = _shift_rows(ext, r)
            for q in range(CONV_HALO // SUBLANES):
                s = SUBLANES * q + r
                if s >= CONV_WIDTH:
                    continue
                k = CONV_WIDTH - 1 - s
                lo = CONV_HALO - SUBLANES * q
                term = rolled[lo:lo + tm, :] * w[k:k + 1, :]
                acc = term if acc is None else acc + term
        c_ref[:, cols] = acc
    u_ref[0:CONV_HALO, :] = u_ref[tm:tm + CONV_HALO, :]

    h = c_ref[...] + dwb_ref[...]
    h = _silu(_layer_norm(h, lng_ref[...], lnb_ref[...]))
    y = alpha * x + _mm(h.astype(BF16), wout_ref[...])
    o_ref[...] = _layer_norm(y, g_ref[...], b_ref[...])


def _conf_layer(x2, w_in, dw, dw_b, ln_g, ln_b, w_out, g, b, *, alpha, seq):
    rows, d = x2.shape
    tm = ROW_TILE
    assert seq % tm == 0 and rows % tm == 0
    win = w_in.astype(BF16).reshape(d, 2, d).transpose(1, 0, 2)
    dwp = jnp.pad(dw, ((0, CONV_HALO - CONV_WIDTH), (0, 0)))
    vec = lambda v: v.reshape(1, d)
    kern = functools.partial(_conf_kernel, alpha=alpha, tiles_per_seq=seq // tm)
    return pl.pallas_call(
        kern,
        out_shape=jax.ShapeDtypeStruct((rows, d), F32),
        grid=(rows // tm,),
        in_specs=[_row_spec(tm, d), _const_spec(win.shape), _const_spec(dwp.shape),
                  _const_spec((1, d)), _const_spec((1, d)), _const_spec((1, d)),
                  _const_spec((d, d)), _const_spec((1, d)), _const_spec((1, d))],
        out_specs=_row_spec(tm, d),
        scratch_shapes=[pltpu.VMEM((CONV_HALO + tm, d), F32), pltpu.VMEM((tm, d), F32)],
        compiler_params=_compiler_params(),
        name="conformer_layer",
    )(x2, win, dwp, vec(dw_b), vec(ln_g), vec(ln_b), w_out.astype(BF16), vec(g), vec(b))


def _sgu_kernel(x_ref, win_ref, lng_ref, lnb_ref, ws_ref, bst_ref, wout_ref, g_ref, b_ref,
                o_ref, s_ref, *, alpha):
    tm, d = x_ref.shape
    heads, chunk, _ = ws_ref.shape
    hd = d // heads

    x = x_ref[...]
    xb = x.astype(BF16)
    u = _gelu_exact(_mm(xb, win_ref[0]))
    v = _gelu_exact(_mm(xb, win_ref[1]))
    vb = _layer_norm(v, lng_ref[...], lnb_ref[...]).astype(BF16)

    t_idx = lax.broadcasted_iota(jnp.int32, (chunk, chunk), 0)
    s_idx = lax.broadcasted_iota(jnp.int32, (chunk, chunk), 1)
    causal = s_idx <= t_idx
    bst = bst_ref[...]
    for h in range(heads):
        wm = jnp.where(causal, ws_ref[h], 0.0).astype(BF16)
        bias = bst[:, h:h + 1]
        for n in range(tm // chunk):
            blk = vb[n * chunk:(n + 1) * chunk, h * hd:(h + 1) * hd]
            s_ref[n * chunk:(n + 1) * chunk, h * hd:(h + 1) * hd] = _mm(wm, blk) + bias

    y = alpha * x + _mm((u * s_ref[...]).astype(BF16), wout_ref[...])
    o_ref[...] = _layer_norm(y, g_ref[...], b_ref[...])


def _sgu_layer(x2, w_in, ln_g, ln_b, ws, bs, w_out, g, b, *, alpha, seq):
    rows, d = x2.shape
    tm = ROW_TILE
    assert tm % SGU_CHUNK == 0 and seq % tm == 0 and rows % tm == 0
    win = w_in.astype(BF16).reshape(d, 2, d).transpose(1, 0, 2)
    bst = jnp.transpose(bs)
    vec = lambda v: v.reshape(1, d)
    kern = functools.partial(_sgu_kernel, alpha=alpha)
    return pl.pallas_call(
        kern,
        out_shape=jax.ShapeDtypeStruct((rows, d), F32),
        grid=(rows // tm,),
        in_specs=[_row_spec(tm, d), _const_spec(win.shape), _const_spec((1, d)), _const_spec((1, d)),
                  _const_spec(ws.shape), _const_spec(bst.shape), _const_spec((d, d)),
                  _const_spec((1, d)), _const_spec((1, d))],
        out_specs=_row_spec(tm, d),
        scratch_shapes=[pltpu.VMEM((tm, d), F32)],
        compiler_params=_compiler_params(),
        name="sgu_layer",
    )(x2, win, vec(ln_g), vec(ln_b), ws, bst, w_out.astype(BF16), vec(g), vec(b))


def _pool_kernel(x_ref, win_ref, wgrp_ref, scale_ref, wout_ref, g_ref, b_ref,
                 o_ref, y_ref, z_ref, *, alpha, tiles_per_seq):
    tm, d = x_ref.shape
    gd = d // len(POOL_WINDOWS)
    tile_in_seq = pl.program_id(0) % tiles_per_seq

    @pl.when(tile_in_seq == 0)
    def _():
        y_ref[0:POOL_HALO, :] = jnp.zeros((POOL_HALO, d), F32)

    x = x_ref[...]
    y_ref[POOL_HALO:, :] = _mm(x.astype(BF16), win_ref[...])

    pos = tile_in_seq * tm + lax.broadcasted_iota(jnp.int32, (tm, 1), 0)
    for gi, w in enumerate(POOL_WINDOWS):
        cols = slice(gi * gd, (gi + 1) * gd)
        ext = y_ref[:, cols]
        wsum, span = ext, 1
        while span < w:
            wsum = wsum + _shift_rows(wsum, span)
            span *= 2
        inv_count = 1.0 / jnp.minimum(pos + 1, w).astype(F32)
        p = wsum[POOL_HALO:, :] * inv_count - ext[POOL_HALO:, :]
        z_ref[:, cols] = _mm(p.astype(BF16), wgrp_ref[gi]) * scale_ref[:, cols]
    y_ref[0:POOL_HALO, :] = y_ref[tm:tm + POOL_HALO, :]

    y = alpha * x + _mm(z_ref[...].astype(BF16), wout_ref[...])
    o_ref[...] = _layer_norm(y, g_ref[...], b_ref[...])


def _pool_layer(x2, w_in, w_grp, scale, w_out, g, b, *, alpha, seq):
    rows, d = x2.shape
    tm = ROW_TILE
    assert seq % tm == 0 and rows % tm == 0
    vec = lambda v: v.reshape(1, d)
    kern = functools.partial(_pool_kernel, alpha=alpha, tiles_per_seq=seq // tm)
    return pl.pallas_call(
        kern,
        out_shape=jax.ShapeDtypeStruct((rows, d), F32),
        grid=(rows // tm,),
        in_specs=[_row_spec(tm, d), _const_spec((d, d)), _const_spec(w_grp.shape), _const_spec((1, d)),
                  _const_spec((d, d)), _const_spec((1, d)), _const_spec((1, d))],
        out_specs=_row_spec(tm, d),
        scratch_shapes=[pltpu.VMEM((POOL_HALO + tm, d), F32), pltpu.VMEM((tm, d), F32)],
        compiler_params=_compiler_params(),
        name="pool_layer",
    )(x2, w_in.astype(BF16), w_grp.astype(BF16), vec(scale), w_out.astype(BF16), vec(g), vec(b))


def kernel(x, a_w_in, a_dw, a_dw_b, a_ln_g, a_ln_b, a_w_out, b_w_in, b_ln_g, b_ln_b, b_ws, b_bs, b_w_out, c_w_in, c_w_grp, c_scale, c_w_out, f_w_up, f_dw, f_w_down, ln1_g, ln1_b, ln2_g, ln2_b):
    bsz, seq, d = x.shape
    depth = f_w_up.shape[0]
    alpha = float((2 * depth) ** 0.25)
    n_mixers = 3
    h = x.reshape(bsz * seq, d)
    for i in range(depth):
        kind, j = i % n_mixers, i // n_mixers
        if kind == 0:
            h = _conf_layer(h, a_w_in[j], a_dw[j], a_dw_b[j], a_ln_g[j], a_ln_b[j], a_w_out[j],
                            ln1_g[i], ln1_b[i], alpha=alpha, seq=seq)
        elif kind == 1:
            h = _sgu_layer(h, b_w_in[j], b_ln_g[j], b_ln_b[j], b_ws[j], b_bs[j], b_w_out[j],
                           ln1_g[i], ln1_b[i], alpha=alpha, seq=seq)
        else:
            h = _pool_layer(h, c_w_in[j], c_w_grp[j], c_scale[j], c_w_out[j],
                            ln1_g[i], ln1_b[i], alpha=alpha, seq=seq)
        h = _ffn_layer(h, f_w_up[i], f_dw[i], f_w_down[i], ln2_g[i], ln2_b[i], alpha=alpha, seq=seq)
    return h.reshape(bsz, seq, d)
```

```python
import functools

import jax
import jax.numpy as jnp
from jax import lax
from jax.experimental import pallas as pl
from jax.experimental.pallas import tpu as pltpu

SUBLANES = 8
LANES = 128
MXU_DIM = 256
VMEM_LIMIT_BYTES = 56 * 1024 * 1024

LN_EPS = 1e-5
CONV_WIDTH = 31
CONV_HALO = 32
FFN_CONV_WIDTH = 3
SGU_CHUNK = 128
POOL_WINDOWS = (2, 4, 8, 16)
POOL_HALO = 16

ROW_TILE = 512
FF_CHUNK = MXU_DIM

F32 = jnp.float32
BF16 = jnp.bfloat16


def _layer_norm(y, g, b):
    mu = jnp.mean(y, axis=-1, keepdims=True)
    d = y - mu
    var = jnp.mean(d * d, axis=-1, keepdims=True)
    return d * lax.rsqrt(var + LN_EPS) * g + b


def _silu(x):
    return x * jax.nn.sigmoid(x)


def _gelu_exact(x):
    return 0.5 * x * (1.0 + lax.erf(x * (2.0 ** -0.5)))


def _shift_rows(e, s):
    return e if s == 0 else pltpu.roll(e, s, 0)


def _mm(a, b):
    return jnp.dot(a, b, preferred_element_type=F32)


def _ffn_kernel(x_ref, wup_ref, dw_ref, wdn_ref, g_ref, b_ref, o_ref, carry_ref, act_ref, *,
                alpha, tiles_per_seq, fc):
    tm = x_ref.shape[0]
    d_ff = wdn_ref.shape[0]

    @pl.when(pl.program_id(0) % tiles_per_seq == 0)
    def _():
        carry_ref[...] = jnp.zeros_like(carry_ref)

    x = x_ref[...]
    xb = x.astype(BF16)

    def conv_cols(col):
        cols = slice(col, col + fc)
        h = _mm(xb, wup_ref[:, cols])
        ext = jnp.concatenate([carry_ref[:, cols], h], axis=0)
        carry_ref[:, cols] = h[tm - SUBLANES:, :]
        out = h * dw_ref[2:3, cols]
        out = out + _shift_rows(ext, 1)[SUBLANES:, :] * dw_ref[1:2, cols]
        out = out + _shift_rows(ext, 2)[SUBLANES:, :] * dw_ref[0:1, cols]
        return out

    for col in range(0, d_ff, fc):
        act_ref[:, col:col + fc] = (_silu(conv_cols(col)) * conv_cols(d_ff + col)).astype(BF16)
    y = alpha * x + _mm(act_ref[...], wdn_ref[...])
    o_ref[...] = _layer_norm(y, g_ref[...], b_ref[...])


def _const_spec(shape):
    nd = len(shape)
    return pl.BlockSpec(shape, lambda i: (0,) * nd, pipeline_mode=pl.Buffered(1))


def _row_spec(tm, d):
    return pl.BlockSpec((tm, d), lambda i: (i, 0))


def _compiler_params():
    return pltpu.CompilerParams(dimension_semantics=("arbitrary",),
                                vmem_limit_bytes=VMEM_LIMIT_BYTES)


def _ffn_layer(x2, w_up, dw, w_down, g, b, *, alpha, seq):
    rows, d = x2.shape
    d_ff = w_down.shape[0]
    tm, fc = ROW_TILE, FF_CHUNK
    assert d_ff % fc == 0 and seq % tm == 0 and rows % tm == 0
    kern = functools.partial(_ffn_kernel, alpha=alpha, tiles_per_seq=seq // tm, fc=fc)
    return pl.pallas_call(
        kern,
        out_shape=jax.ShapeDtypeStruct((rows, d), F32),
        grid=(rows // tm,),
        in_specs=[_row_spec(tm, d), _const_spec(w_up.shape), _const_spec(dw.shape),
                  _const_spec(w_down.shape), _const_spec((1, d)), _const_spec((1, d))],
        out_specs=_row_spec(tm, d),
        scratch_shapes=[pltpu.VMEM((SUBLANES, 2 * d_ff), F32), pltpu.VMEM((tm, d_ff), BF16)],
        compiler_params=_compiler_params(),
        name="ffn_layer",
    )(x2, w_up.astype(BF16), dw, w_down.astype(BF16), g.reshape(1, d), b.reshape(1, d))


def _conf_kernel(x_ref, win_ref, dw_ref, dwb_ref, lng_ref, lnb_ref, wout_ref, g_ref, b_ref,
                 o_ref, u_ref, c_ref, *, alpha, tiles_per_seq):
    tm, d = x_ref.shape

    @pl.when(pl.program_id(0) % tiles_per_seq == 0)
    def _():
        u_ref[0:CONV_HALO, :] = jnp.zeros((CONV_HALO, d), F32)

    x = x_ref[...]
    xb = x.astype(BF16)
    a = _mm(xb, win_ref[:, 0:d])
    gate = _mm(xb, win_ref[:, d:2 * d])
    u_ref[CONV_HALO:, :] = a * jax.nn.sigmoid(gate)

    for cb in range(d // LANES):
        cols = slice(cb * LANES, (cb + 1) * LANES)
        ext = u_ref[:, cols]
        w = dw_ref[:, cols]
        acc = None
        for r in range(SUBLANES):
            rolled = _shift_rows(ext, r)
            for q in range(CONV_HALO // SUBLANES):
                s = SUBLANES * q + r
                if s >= CONV_WIDTH:
                    continue
                k = CONV_WIDTH - 1 - s
                lo = CONV_HALO - SUBLANES * q
                term = rolled[lo:lo + tm, :] * w[k:k + 1, :]
                acc = term if acc is None else acc + term
        c_ref[:, cols] = acc
    u_ref[0:CONV_HALO, :] = u_ref[tm:tm + CONV_HALO, :]

    h = c_ref[...] + dwb_ref[...]
    h = _silu(_layer_norm(h, lng_ref[...], lnb_ref[...]))
    y = alpha * x + _mm(h.astype(BF16), wout_ref[...])
    o_ref[...] = _layer_norm(y, g_ref[...], b_ref[...])


def _conf_layer(x2, w_in, dw, dw_b, ln_g, ln_b, w_out, g, b, *, alpha, seq):
    rows, d = x2.shape
    tm = ROW_TILE
    assert seq % tm == 0 and rows % tm == 0
    win = w_in.astype(BF16)
    dwp = jnp.pad(dw, ((0, CONV_HALO - CONV_WIDTH), (0, 0)))
    vec = lambda v: v.reshape(1, d)
    kern = functools.partial(_conf_kernel, alpha=alpha, tiles_per_seq=seq // tm)
    return pl.pallas_call(
        kern,
        out_shape=jax.ShapeDtypeStruct((rows, d), F32),
        grid=(rows // tm,),
        in_specs=[_row_spec(tm, d), _const_spec(win.shape), _const_spec(dwp.shape),
                  _const_spec((1, d)), _const_spec((1, d)), _const_spec((1, d)),
                  _const_spec((d, d)), _const_spec((1, d)), _const_spec((1, d))],
        out_specs=_row_spec(tm, d),
        scratch_shapes=[pltpu.VMEM((CONV_HALO + tm, d), F32), pltpu.VMEM((tm, d), F32)],
        compiler_params=_compiler_params(),
        name="conformer_layer",
    )(x2, win, dwp, vec(dw_b), vec(ln_g), vec(ln_b), w_out.astype(BF16), vec(g), vec(b))


def _sgu_kernel(x_ref, win_ref, lng_ref, lnb_ref, ws_ref, bst_ref, wout_ref, g_ref, b_ref,
                o_ref, s_ref, *, alpha):
    tm, d = x_ref.shape
    heads, chunk, _ = ws_ref.shape
    hd = d // heads

    x = x_ref[...]
    xb = x.astype(BF16)
    u = _gelu_exact(_mm(xb, win_ref[:, 0:d]))
    v = _gelu_exact(_mm(xb, win_ref[:, d:2 * d]))
    vb = _layer_norm(v, lng_ref[...], lnb_ref[...]).astype(BF16)

    t_idx = lax.broadcasted_iota(jnp.int32, (chunk, chunk), 0)
    s_idx = lax.broadcasted_iota(jnp.int32, (chunk, chunk), 1)
    causal = s_idx <= t_idx
    bst = bst_ref[...]
    for h in range(heads):
        wm = jnp.where(causal, ws_ref[h], 0.0).astype(BF16)
        bias = bst[:, h:h + 1]
        for n in range(tm // chunk):
            blk = vb[n * chunk:(n + 1) * chunk, h * hd:(h + 1) * hd]
            s_ref[n * chunk:(n + 1) * chunk, h * hd:(h + 1) * hd] = _mm(wm, blk) + bias

    y = alpha * x + _mm((u * s_ref[...]).astype(BF16), wout_ref[...])
    o_ref[...] = _layer_norm(y, g_ref[...], b_ref[...])


def _sgu_layer(x2, w_in, ln_g, ln_b, ws, bs, w_out, g, b, *, alpha, seq):
    rows, d = x2.shape
    tm = ROW_TILE
    assert tm % SGU_CHUNK == 0 and seq % tm == 0 and rows % tm == 0
    win = w_in.astype(BF16)
    bst = jnp.transpose(bs)
    vec = lambda v: v.reshape(1, d)
    kern = functools.partial(_sgu_kernel, alpha=alpha)
    return pl.pallas_call(
        kern,
        out_shape=jax.ShapeDtypeStruct((rows, d), F32),
        grid=(rows // tm,),
        in_specs=[_row_spec(tm, d), _const_spec(win.shape), _const_spec((1, d)), _const_spec((1, d)),
                  _const_spec(ws.shape), _const_spec(bst.shape), _const_spec((d, d)),
                  _const_spec((1, d)), _const_spec((1, d))],
        out_specs=_row_spec(tm, d),
        scratch_shapes=[pltpu.VMEM((tm, d), F32)],
        compiler_params=_compiler_params(),
        name="sgu_layer",
    )(x2, win, vec(ln_g), vec(ln_b), ws, bst, w_out.astype(BF16), vec(g), vec(b))


def _pool_kernel(x_ref, win_ref, wgrp_ref, scale_ref, wout_ref, g_ref, b_ref,
                 o_ref, y_ref, z_ref, *, alpha, tiles_per_seq):
    tm, d = x_ref.shape
    gd = d // len(POOL_WINDOWS)
    tile_in_seq = pl.program_id(0) % tiles_per_seq

    @pl.when(tile_in_seq == 0)
    def _():
        y_ref[0:POOL_HALO, :] = jnp.zeros((POOL_HALO, d), F32)

    x = x_ref[...]
    y_ref[POOL_HALO:, :] = _mm(x.astype(BF16), win_ref[...])

    pos = tile_in_seq * tm + lax.broadcasted_iota(jnp.int32, (tm, 1), 0)
    for gi, w in enumerate(POOL_WINDOWS):
        cols = slice(gi * gd, (gi + 1) * gd)
        ext = y_ref[:, cols]
        wsum, span = ext, 1
        while span < w:
            wsum = wsum + _shift_rows(wsum, span)
            span *= 2
        inv_count = 1.0 / jnp.minimum(pos + 1, w).astype(F32)
        p = wsum[POOL_HALO:, :] * inv_count - ext[POOL_HALO:, :]
        z_ref[:, cols] = _mm(p.astype(BF16), wgrp_ref[gi]) * scale_ref[:, cols]
    y_ref[0:POOL_HALO, :] = y_ref[tm:tm + POOL_HALO, :]

    y = alpha * x + _mm(z_ref[...].astype(BF16), wout_ref[...])
    o_ref[...] = _layer_norm(y, g_ref[...], b_ref[...])


def _pool_layer(x2, w_in, w_grp, scale, w_out, g, b, *, alpha, seq):
    rows, d = x2.shape
    tm = ROW_TILE
    assert seq % tm == 0 and rows % tm == 0
    vec = lambda v: v.reshape(1, d)
    kern = functools.partial(_pool_kernel, alpha=alpha, tiles_per_seq=seq // tm)
    return pl.pallas_call(
        kern,
        out_shape=jax.ShapeDtypeStruct((rows, d), F32),
        grid=(rows // tm,),
        in_specs=[_row_spec(tm, d), _const_spec((d, d)), _const_spec(w_grp.shape), _const_spec((1, d)),
                  _const_spec((d, d)), _const_spec((1, d)), _const_spec((1, d))],
        out_specs=_row_spec(tm, d),
        scratch_shapes=[pltpu.VMEM((POOL_HALO + tm, d), F32), pltpu.VMEM((tm, d), F32)],
        compiler_params=_compiler_params(),
        name="pool_layer",
    )(x2, w_in.astype(BF16), w_grp.astype(BF16), vec(scale), w_out.astype(BF16), vec(g), vec(b))


def kernel(x, a_w_in, a_dw, a_dw_b, a_ln_g, a_ln_b, a_w_out, b_w_in, b_ln_g, b_ln_b, b_ws, b_bs, b_w_out, c_w_in, c_w_grp, c_scale, c_w_out, f_w_up, f_dw, f_w_down, ln1_g, ln1_b, ln2_g, ln2_b):
    bsz, seq, d = x.shape
    depth = f_w_up.shape[0]
    alpha = float((2 * depth) ** 0.25)
    n_mixers = 3
    h = x.reshape(bsz * seq, d)
    for i in range(depth):
        kind, j = i % n_mixers, i // n_mixers
        if kind == 0:
            h = _conf_layer(h, a_w_in[j], a_dw[j], a_dw_b[j], a_ln_g[j], a_ln_b[j], a_w_out[j],
                            ln1_g[i], ln1_b[i], alpha=alpha, seq=seq)
        elif kind == 1:
            h = _sgu_layer(h, b_w_in[j], b_ln_g[j], b_ln_b[j], b_ws[j], b_bs[j], b_w_out[j],
                           ln1_g[i], ln1_b[i], alpha=alpha, seq=seq)
        else:
            h = _pool_layer(h, c_w_in[j], c_w_grp[j], c_scale[j], c_w_out[j],
                            ln1_g[i], ln1_b[i], alpha=alpha, seq=seq)
        h = _ffn_layer(h, f_w_up[i], f_dw[i], f_w_down[i], ln2_g[i], ln2_b[i], alpha=alpha, seq=seq)
    return h.reshape(bsz, seq, d)
```

```python
import functools

import jax
import jax.numpy as jnp
from jax import lax
from jax.experimental import pallas as pl
from jax.experimental.pallas import tpu as pltpu

SUBLANES = 8
LANES = 128
MXU_DIM = 256
VMEM_LIMIT_BYTES = 56 * 1024 * 1024

LN_EPS = 1e-5
CONV_WIDTH = 31
CONV_HALO = 32
SGU_CHUNK = 128
POOL_WINDOWS = (2, 4, 8, 16)
POOL_HALO = 16

ROW_TILE = 512
FF_CHUNK = MXU_DIM

F32 = jnp.float32
BF16 = jnp.bfloat16


def _layer_norm(y, g, b):
    mu = jnp.mean(y, axis=-1, keepdims=True)
    d = y - mu
    var = jnp.mean(d * d, axis=-1, keepdims=True)
    return d * lax.rsqrt(var + LN_EPS) * g + b


def _silu(x):
    return x * jax.nn.sigmoid(x)


def _gelu_exact(x):
    return 0.5 * x * (1.0 + lax.erf(x * (2.0 ** -0.5)))


def _shift_rows(e, s):
    return e if s == 0 else pltpu.roll(e, s, 0)


def _mm(a, b):
    return jnp.dot(a, b, preferred_element_type=F32)


def _halves(tm):
    half = tm // 2
    return [slice(0, half), slice(half, tm)]


def _const_spec(shape):
    nd = len(shape)
    return pl.BlockSpec(shape, lambda i: (0,) * nd, pipeline_mode=pl.Buffered(1))


def _layer_spec(stack, j):
    nd = stack.ndim - 1
    return pl.BlockSpec((None,) + stack.shape[1:], lambda i: (j,) + (0,) * nd, pipeline_mode=pl.Buffered(1))


def _row_spec(tm, d):
    return pl.BlockSpec((tm, d), lambda i: (i, 0))


def _compiler_params():
    return pltpu.CompilerParams(dimension_semantics=("arbitrary",), vmem_limit_bytes=VMEM_LIMIT_BYTES)


def _ffn_kernel(x_ref, wup_ref, dw_ref, wdn_ref, g_ref, b_ref, o_ref, carry_ref, act_ref, *,
                alpha, tiles_per_seq, fc):
    tm = x_ref.shape[0]
    d_ff = wdn_ref.shape[0]

    @pl.when(pl.program_id(0) % tiles_per_seq == 0)
    def _():
        carry_ref[...] = jnp.zeros_like(carry_ref)

    xb = x_ref[...].astype(BF16)

    def conv_cols(col):
        cols = slice(col, col + fc)
        h = _mm(xb, wup_ref[:, cols])
        ext = jnp.concatenate([carry_ref[:, cols], h], axis=0)
        carry_ref[:, cols] = h[tm - SUBLANES:, :]
        out = h * dw_ref[2:3, cols]
        out = out + _shift_rows(ext, 1)[SUBLANES:, :] * dw_ref[1:2, cols]
        out = out + _shift_rows(ext, 2)[SUBLANES:, :] * dw_ref[0:1, cols]
        return out

    for col in range(0, d_ff, fc):
        act_ref[:, col:col + fc] = (_silu(conv_cols(col)) * conv_cols(d_ff + col)).astype(BF16)
    for rows in _halves(tm):
        y = alpha * x_ref[rows, :] + _mm(act_ref[rows, :], wdn_ref[...])
        o_ref[rows, :] = _layer_norm(y, g_ref[...], b_ref[...])


def _ffn_layer(x2, w_up, dw, w_down, g, b, i, *, alpha, seq):
    rows, d = x2.shape
    d_ff = w_down.shape[1]
    tm, fc = ROW_TILE, FF_CHUNK
    assert d_ff % fc == 0 and seq % tm == 0 and rows % tm == 0
    kern = functools.partial(_ffn_kernel, alpha=alpha, tiles_per_seq=seq // tm, fc=fc)
    return pl.pallas_call(
        kern,
        out_shape=jax.ShapeDtypeStruct((rows, d), F32),
        grid=(rows // tm,),
        in_specs=[_row_spec(tm, d), _layer_spec(w_up, i), _layer_spec(dw, i), _layer_spec(w_down, i),
                  _const_spec((1, d)), _const_spec((1, d))],
        out_specs=_row_spec(tm, d),
        scratch_shapes=[pltpu.VMEM((SUBLANES, 2 * d_ff), F32), pltpu.VMEM((tm, d_ff), BF16)],
        compiler_params=_compiler_params(),
        name="ffn_layer",
    )(x2, w_up, dw, w_down, g.reshape(1, d), b.reshape(1, d))


def _conf_kernel(x_ref, win_ref, dw_ref, dwb_ref, lng_ref, lnb_ref, wout_ref, g_ref, b_ref,
                 o_ref, u_ref, c_ref, hb_ref, *, alpha, tiles_per_seq):
    tm, d = x_ref.shape

    @pl.when(pl.program_id(0) % tiles_per_seq == 0)
    def _():
        u_ref[0:CONV_HALO, :] = jnp.zeros((CONV_HALO, d), F32)

    xb = x_ref[...].astype(BF16)
    for c in range(0, d, MXU_DIM):
        a = _mm(xb, win_ref[:, c:c + MXU_DIM])
        gate = _mm(xb, win_ref[:, d + c:d + c + MXU_DIM])
        u_ref[CONV_HALO:, c:c + MXU_DIM] = a * jax.nn.sigmoid(gate)

    for c in range(0, d, LANES):
        cols = slice(c, c + LANES)
        ext = u_ref[:, cols]
        acc = None
        for r in range(SUBLANES):
            rolled = _shift_rows(ext, r)
            for q in range(CONV_HALO // SUBLANES):
                s = SUBLANES * q + r
                if s >= CONV_WIDTH:
                    continue
                k = CONV_WIDTH - 1 - s
                lo = CONV_HALO - SUBLANES * q
                term = rolled[lo:lo + tm, :] * dw_ref[k:k + 1, cols]
                acc = term if acc is None else acc + term
        c_ref[:, cols] = acc
    u_ref[0:CONV_HALO, :] = u_ref[tm:tm + CONV_HALO, :]

    for rows in _halves(tm):
        h = c_ref[rows, :] + dwb_ref[...]
        hb_ref[rows, :] = _silu(_layer_norm(h, lng_ref[...], lnb_ref[...])).astype(BF16)
        o_ref[rows, :] = alpha * x_ref[rows, :] + _mm(hb_ref[rows, :], wout_ref[...])
    for rows in _halves(tm):
        o_ref[rows, :] = _layer_norm(o_ref[rows, :], g_ref[...], b_ref[...])


def _conf_layer(x2, w_in, dw, dw_b, ln_g, ln_b, w_out, g, b, j, *, alpha, seq):
    rows, d = x2.shape
    tm = ROW_TILE
    assert seq % tm == 0 and rows % tm == 0 and d % MXU_DIM == 0
    vec = lambda v: v.reshape(1, d)
    kern = functools.partial(_conf_kernel, alpha=alpha, tiles_per_seq=seq // tm)
    return pl.pallas_call(
        kern,
        out_shape=jax.ShapeDtypeStruct((rows, d), F32),
        grid=(rows // tm,),
        in_specs=[_row_spec(tm, d), _layer_spec(w_in, j), _layer_spec(dw, j),
                  _const_spec((1, d)), _const_spec((1, d)), _const_spec((1, d)),
                  _layer_spec(w_out, j), _const_spec((1, d)), _const_spec((1, d))],
        out_specs=_row_spec(tm, d),
        scratch_shapes=[pltpu.VMEM((CONV_HALO + tm, d), F32), pltpu.VMEM((tm, d), F32), pltpu.VMEM((tm, d), BF16)],
        compiler_params=_compiler_params(),
        name="conformer_layer",
    )(x2, w_in, dw, vec(dw_b), vec(ln_g), vec(ln_b), w_out, vec(g), vec(b))


def _sgu_kernel(x_ref, win_ref, lng_ref, lnb_ref, ws_ref, bst_ref, wout_ref, g_ref, b_ref,
                o_ref, u_ref, v_ref, vb_ref, us_ref, *, alpha):
    tm, d = x_ref.shape
    heads, chunk, _ = ws_ref.shape
    hd = d // heads
    xb = x_ref[...].astype(BF16)
    col_blocks = [slice(c, c + MXU_DIM) for c in range(0, d, MXU_DIM)]

    for cols in col_blocks:
        v_ref[:, cols] = _gelu_exact(_mm(xb, win_ref[:, slice(d + cols.start, d + cols.stop)]))
    for cols in col_blocks:
        u_ref[:, cols] = _gelu_exact(_mm(xb, win_ref[:, cols]))
    for rows in _halves(tm):
        vb_ref[rows, :] = _layer_norm(v_ref[rows, :], lng_ref[...], lnb_ref[...]).astype(BF16)

    t_idx = lax.broadcasted_iota(jnp.int32, (chunk, chunk), 0)
    s_idx = lax.broadcasted_iota(jnp.int32, (chunk, chunk), 1)
    causal = s_idx <= t_idx
    for h in range(heads):
        wm = jnp.where(causal, ws_ref[h], 0.0).astype(BF16)
        bias = bst_ref[:, h:h + 1]
        cols = slice(h * hd, (h + 1) * hd)
        for n in range(tm // chunk):
            rows = slice(n * chunk, (n + 1) * chunk)
            s = _mm(wm, vb_ref[rows, cols]) + bias
            us_ref[rows, cols] = (u_ref[rows, cols] * s).astype(BF16)

    for rows in _halves(tm):
        y = alpha * x_ref[rows, :] + _mm(us_ref[rows, :], wout_ref[...])
        o_ref[rows, :] = _layer_norm(y, g_ref[...], b_ref[...])


def _sgu_layer(x2, w_in, ln_g, ln_b, ws, bs, w_out, g, b, j, *, alpha, seq):
    rows, d = x2.shape
    tm = ROW_TILE
    assert tm % SGU_CHUNK == 0 and seq % tm == 0 and rows % tm == 0
    bst = jnp.transpose(bs)
    vec = lambda v: v.reshape(1, d)
    kern = functools.partial(_sgu_kernel, alpha=alpha)
    return pl.pallas_call(
        kern,
        out_shape=jax.ShapeDtypeStruct((rows, d), F32),
        grid=(rows // tm,),
        in_specs=[_row_spec(tm, d), _layer_spec(w_in, j), _const_spec((1, d)), _const_spec((1, d)),
                  _layer_spec(ws, j), _const_spec(bst.shape), _layer_spec(w_out, j),
                  _const_spec((1, d)), _const_spec((1, d))],
        out_specs=_row_spec(tm, d),
        scratch_shapes=[pltpu.VMEM((tm, d), F32), pltpu.VMEM((tm, d), F32), pltpu.VMEM((tm, d), BF16),
                        pltpu.VMEM((tm, d), BF16)],
        compiler_params=_compiler_params(),
        name="sgu_layer",
    )(x2, w_in, vec(ln_g), vec(ln_b), ws, bst, w_out, vec(g), vec(b))


def _pool_kernel(x_ref, win_ref, wgrp_ref, scale_ref, wout_ref, g_ref, b_ref,
                 o_ref, p_ref, zb_ref, *, alpha, tiles_per_seq):
    tm, d = x_ref.shape
    gd = d // len(POOL_WINDOWS)
    tile_in_seq = pl.program_id(0) % tiles_per_seq

    @pl.when(tile_in_seq == 0)
    def _():
        p_ref[0:POOL_HALO, :] = jnp.zeros((POOL_HALO, d), F32)

    xb = x_ref[...].astype(BF16)
    pos = tile_in_seq * tm + lax.broadcasted_iota(jnp.int32, (tm, 1), 0)
    for gi, w in enumerate(POOL_WINDOWS):
        cols = slice(gi * gd, (gi + 1) * gd)
        p_ref[POOL_HALO:, cols] = _mm(xb, win_ref[:, cols])
    for gi, w in enumerate(POOL_WINDOWS):
        cols = slice(gi * gd, (gi + 1) * gd)
        ext = p_ref[:, cols]
        p_ref[0:POOL_HALO, cols] = ext[tm:tm + POOL_HALO, :]
        wsum, span = ext, 1
        while span < w:
            wsum = wsum + _shift_rows(wsum, span)
            span *= 2
        inv_count = 1.0 / jnp.minimum(pos + 1, w).astype(F32)
        p = wsum[POOL_HALO:, :] * inv_count - ext[POOL_HALO:, :]
        zb_ref[:, cols] = (_mm(p.astype(BF16), wgrp_ref[gi]) * scale_ref[:, cols]).astype(BF16)

    for rows in _halves(tm):
        y = alpha * x_ref[rows, :] + _mm(zb_ref[rows, :], wout_ref[...])
        o_ref[rows, :] = _layer_norm(y, g_ref[...], b_ref[...])


def _pool_layer(x2, w_in, w_grp, scale, w_out, g, b, j, *, alpha, seq):
    rows, d = x2.shape
    tm = ROW_TILE
    assert seq % tm == 0 and rows % tm == 0
    vec = lambda v: v.reshape(1, d)
    kern = functools.partial(_pool_kernel, alpha=alpha, tiles_per_seq=seq // tm)
    return pl.pallas_call(
        kern,
        out_shape=jax.ShapeDtypeStruct((rows, d), F32),
        grid=(rows // tm,),
        in_specs=[_row_spec(tm, d), _layer_spec(w_in, j), _layer_spec(w_grp, j), _const_spec((1, d)),
                  _layer_spec(w_out, j), _const_spec((1, d)), _const_spec((1, d))],
        out_specs=_row_spec(tm, d),
        scratch_shapes=[pltpu.VMEM((POOL_HALO + tm, d), F32), pltpu.VMEM((tm, d), BF16)],
        compiler_params=_compiler_params(),
        name="pool_layer",
    )(x2, w_in, w_grp, vec(scale), w_out, vec(g), vec(b))


def kernel(x, a_w_in, a_dw, a_dw_b, a_ln_g, a_ln_b, a_w_out, b_w_in, b_ln_g, b_ln_b, b_ws, b_bs, b_w_out, c_w_in, c_w_grp, c_scale, c_w_out, f_w_up, f_dw, f_w_down, ln1_g, ln1_b, ln2_g, ln2_b):
    bsz, seq, d = x.shape
    depth = f_w_up.shape[0]
    alpha = float((2 * depth) ** 0.25)
    n_mixers = 3
    a_w_in, a_w_out, b_w_in, b_w_out, c_w_in, c_w_grp, c_w_out, f_w_up, f_w_down = (
        w.astype(BF16) for w in (a_w_in, a_w_out, b_w_in, b_w_out, c_w_in, c_w_grp, c_w_out, f_w_up, f_w_down))
    h = x.reshape(bsz * seq, d)
    for i in range(depth):
        kind, j = i % n_mixers, i // n_mixers
        if kind == 0:
            h = _conf_layer(h, a_w_in, a_dw, a_dw_b[j], a_ln_g[j], a_ln_b[j], a_w_out,
                            ln1_g[i], ln1_b[i], j, alpha=alpha, seq=seq)
        elif kind == 1:
            h = _sgu_layer(h, b_w_in, b_ln_g[j], b_ln_b[j], b_ws, b_bs[j], b_w_out,
                           ln1_g[i], ln1_b[i], j, alpha=alpha, seq=seq)
        else:
            h = _pool_layer(h, c_w_in, c_w_grp, c_scale[j], c_w_out,
                            ln1_g[i], ln1_b[i], j, alpha=alpha, seq=seq)
        h = _ffn_layer(h, f_w_up, f_dw, f_w_down, ln2_g[i], ln2_b[i], i, alpha=alpha, seq=seq)
    return h.reshape(bsz, seq, d)
```

```python
import functools

import jax
import jax.numpy as jnp
from jax import lax
from jax.experimental import pallas as pl
from jax.experimental.pallas import tpu as pltpu

SUBLANES = 8
LANES = 128
MXU_DIM = 256
VMEM_LIMIT_BYTES = 56 * 1024 * 1024

LN_EPS = 1e-5
CONV_WIDTH = 31
CONV_HALO = 32
CONV_SUB = 4
SGU_CHUNK = 128
POOL_WINDOWS = (2, 4, 8, 16)
POOL_HALO = 16

ROW_TILE = 512
FF_CHUNK = MXU_DIM

F32 = jnp.float32
BF16 = jnp.bfloat16


def _layer_norm(y, g, b):
    mu = jnp.mean(y, axis=-1, keepdims=True)
    d = y - mu
    var = jnp.mean(d * d, axis=-1, keepdims=True)
    return d * lax.rsqrt(var + LN_EPS) * g + b


def _silu(x):
    return x * jax.nn.sigmoid(x)


def _gelu_exact(x):
    return 0.5 * x * (1.0 + lax.erf(x * (2.0 ** -0.5)))


def _shift_rows(e, s):
    return e if s == 0 else pltpu.roll(e, s, 0)


def _mm(a, b):
    return jnp.dot(a, b, preferred_element_type=F32)


def _halves(tm):
    half = tm // 2
    return [slice(0, half), slice(half, tm)]


def _const_spec(shape):
    nd = len(shape)
    return pl.BlockSpec(shape, lambda i: (0,) * nd, pipeline_mode=pl.Buffered(1))


def _layer_spec(stack, j):
    nd = stack.ndim - 1
    return pl.BlockSpec((None,) + stack.shape[1:], lambda i: (j,) + (0,) * nd, pipeline_mode=pl.Buffered(1))


def _row_spec(tm, d):
    return pl.BlockSpec((tm, d), lambda i: (i, 0))


def _compiler_params():
    return pltpu.CompilerParams(dimension_semantics=("arbitrary",), vmem_limit_bytes=VMEM_LIMIT_BYTES)


def _ffn_kernel(x_ref, wup_ref, dw_ref, wdn_ref, g_ref, b_ref, o_ref, carry_ref, act_ref, *,
                alpha, tiles_per_seq, fc):
    tm = x_ref.shape[0]
    d_ff = wdn_ref.shape[0]

    @pl.when(pl.program_id(0) % tiles_per_seq == 0)
    def _():
        carry_ref[...] = jnp.zeros_like(carry_ref)

    xb = x_ref[...].astype(BF16)

    def conv_cols(col):
        cols = slice(col, col + fc)
        h = _mm(xb, wup_ref[:, cols])
        ext = jnp.concatenate([carry_ref[:, cols], h], axis=0)
        carry_ref[:, cols] = h[tm - SUBLANES:, :]
        out = h * dw_ref[2:3, cols]
        out = out + _shift_rows(ext, 1)[SUBLANES:, :] * dw_ref[1:2, cols]
        out = out + _shift_rows(ext, 2)[SUBLANES:, :] * dw_ref[0:1, cols]
        return out

    for col in range(0, d_ff, fc):
        act_ref[:, col:col + fc] = (_silu(conv_cols(col)) * conv_cols(d_ff + col)).astype(BF16)
    for rows in _halves(tm):
        y = alpha * x_ref[rows, :] + _mm(act_ref[rows, :], wdn_ref[...])
        o_ref[rows, :] = _layer_norm(y, g_ref[...], b_ref[...])


def _ffn_layer(x2, w_up, dw, w_down, g, b, i, *, alpha, seq):
    rows, d = x2.shape
    d_ff = w_down.shape[1]
    tm, fc = ROW_TILE, FF_CHUNK
    assert d_ff % fc == 0 and seq % tm == 0 and rows % tm == 0
    kern = functools.partial(_ffn_kernel, alpha=alpha, tiles_per_seq=seq // tm, fc=fc)
    return pl.pallas_call(
        kern,
        out_shape=jax.ShapeDtypeStruct((rows, d), F32),
        grid=(rows // tm,),
        in_specs=[_row_spec(tm, d), _layer_spec(w_up, i), _layer_spec(dw, i), _layer_spec(w_down, i),
                  _const_spec((1, d)), _const_spec((1, d))],
        out_specs=_row_spec(tm, d),
        scratch_shapes=[pltpu.VMEM((SUBLANES, 2 * d_ff), F32), pltpu.VMEM((tm, d_ff), BF16)],
        compiler_params=_compiler_params(),
        name="ffn_layer",
    )(x2, w_up, dw, w_down, g.reshape(1, d), b.reshape(1, d))


def _conf_kernel(x_ref, win_ref, dw_ref, dwb_ref, lng_ref, lnb_ref, wout_ref, g_ref, b_ref,
                 o_ref, u_ref, c_ref, hb_ref, *, alpha, tiles_per_seq):
    tm, d = x_ref.shape

    @pl.when(pl.program_id(0) % tiles_per_seq == 0)
    def _():
        u_ref[0:CONV_HALO, :] = jnp.zeros((CONV_HALO, d), F32)

    xb = x_ref[...].astype(BF16)
    for c in range(0, d, MXU_DIM):
        a = _mm(xb, win_ref[:, c:c + MXU_DIM])
        gate = _mm(xb, win_ref[:, d + c:d + c + MXU_DIM])
        u_ref[CONV_HALO:, c:c + MXU_DIM] = a * jax.nn.sigmoid(gate)

    for col in range(0, d, LANES):
        cols = slice(col, col + LANES)
        ext = u_ref[:, cols]
        rolled = [_shift_rows(ext, b) for b in range(CONV_SUB)]
        sums = [None, None]
        for s in range(CONV_WIDTH):
            q, rem = divmod(s, SUBLANES)
            c, b = divmod(rem, CONV_SUB)
            lo = CONV_HALO - SUBLANES * (q + c)
            n = tm + SUBLANES * c
            term = rolled[b][lo:lo + n, :] * dw_ref[CONV_WIDTH - 1 - s:CONV_WIDTH - s, cols]
            sums[c] = term if sums[c] is None else sums[c] + term
        c_ref[:, cols] = sums[0] + _shift_rows(sums[1], CONV_SUB)[SUBLANES:, :]
    u_ref[0:CONV_HALO, :] = u_ref[tm:tm + CONV_HALO, :]

    for rows in _halves(tm):
        h = c_ref[rows, :] + dwb_ref[...]
        hb_ref[rows, :] = _silu(_layer_norm(h, lng_ref[...], lnb_ref[...])).astype(BF16)
        o_ref[rows, :] = alpha * x_ref[rows, :] + _mm(hb_ref[rows, :], wout_ref[...])
    for rows in _halves(tm):
        o_ref[rows, :] = _layer_norm(o_ref[rows, :], g_ref[...], b_ref[...])


def _conf_layer(x2, w_in, dw, dw_b, ln_g, ln_b, w_out, g, b, j, *, alpha, seq):
    rows, d = x2.shape
    tm = ROW_TILE
    assert seq % tm == 0 and rows % tm == 0 and d % MXU_DIM == 0
    vec = lambda v: v.reshape(1, d)
    kern = functools.partial(_conf_kernel, alpha=alpha, tiles_per_seq=seq // tm)
    return pl.pallas_call(
        kern,
        out_shape=jax.ShapeDtypeStruct((rows, d), F32),
        grid=(rows // tm,),
        in_specs=[_row_spec(tm, d), _layer_spec(w_in, j), _layer_spec(dw, j),
                  _const_spec((1, d)), _const_spec((1, d)), _const_spec((1, d)),
                  _layer_spec(w_out, j), _const_spec((1, d)), _const_spec((1, d))],
        out_specs=_row_spec(tm, d),
        scratch_shapes=[pltpu.VMEM((CONV_HALO + tm, d), F32), pltpu.VMEM((tm, d), F32), pltpu.VMEM((tm, d), BF16)],
        compiler_params=_compiler_params(),
        name="conformer_layer",
    )(x2, w_in, dw, vec(dw_b), vec(ln_g), vec(ln_b), w_out, vec(g), vec(b))


def _sgu_kernel(x_ref, win_ref, lng_ref, lnb_ref, ws_ref, bst_ref, wout_ref, g_ref, b_ref,
                o_ref, u_ref, v_ref, vb_ref, us_ref, *, alpha):
    tm, d = x_ref.shape
    heads, chunk, _ = ws_ref.shape
    hd = d // heads
    xb = x_ref[...].astype(BF16)
    col_blocks = [slice(c, c + MXU_DIM) for c in range(0, d, MXU_DIM)]

    for cols in col_blocks:
        v_ref[:, cols] = _gelu_exact(_mm(xb, win_ref[:, slice(d + cols.start, d + cols.stop)]))
    for cols in col_blocks:
        u_ref[:, cols] = _gelu_exact(_mm(xb, win_ref[:, cols]))
    for rows in _halves(tm):
        vb_ref[rows, :] = _layer_norm(v_ref[rows, :], lng_ref[...], lnb_ref[...]).astype(BF16)

    t_idx = lax.broadcasted_iota(jnp.int32, (chunk, chunk), 0)
    s_idx = lax.broadcasted_iota(jnp.int32, (chunk, chunk), 1)
    causal = s_idx <= t_idx
    for h in range(heads):
        wm = jnp.where(causal, ws_ref[h], 0.0).astype(BF16)
        bias = bst_ref[:, h:h + 1]
        cols = slice(h * hd, (h + 1) * hd)
        for n in range(tm // chunk):
            rows = slice(n * chunk, (n + 1) * chunk)
            s = _mm(wm, vb_ref[rows, cols]) + bias
            us_ref[rows, cols] = (u_ref[rows, cols] * s).astype(BF16)

    for rows in _halves(tm):
        y = alpha * x_ref[rows, :] + _mm(us_ref[rows, :], wout_ref[...])
        o_ref[rows, :] = _layer_norm(y, g_ref[...], b_ref[...])


def _sgu_layer(x2, w_in, ln_g, ln_b, ws, bs, w_out, g, b, j, *, alpha, seq):
    rows, d = x2.shape
    tm = ROW_TILE
    assert tm % SGU_CHUNK == 0 and seq % tm == 0 and rows % tm == 0
    bst = jnp.transpose(bs)
    vec = lambda v: v.reshape(1, d)
    kern = functools.partial(_sgu_kernel, alpha=alpha)
    return pl.pallas_call(
        kern,
        out_shape=jax.ShapeDtypeStruct((rows, d), F32),
        grid=(rows // tm,),
        in_specs=[_row_spec(tm, d), _layer_spec(w_in, j), _const_spec((1, d)), _const_spec((1, d)),
                  _layer_spec(ws, j), _const_spec(bst.shape), _layer_spec(w_out, j),
                  _const_spec((1, d)), _const_spec((1, d))],
        out_specs=_row_spec(tm, d),
        scratch_shapes=[pltpu.VMEM((tm, d), F32), pltpu.VMEM((tm, d), F32), pltpu.VMEM((tm, d), BF16),
                        pltpu.VMEM((tm, d), BF16)],
        compiler_params=_compiler_params(),
        name="sgu_layer",
    )(x2, w_in, vec(ln_g), vec(ln_b), ws, bst, w_out, vec(g), vec(b))


def _pool_kernel(x_ref, win_ref, wgrp_ref, scale_ref, wout_ref, g_ref, b_ref,
                 o_ref, p_ref, zb_ref, *, alpha, tiles_per_seq):
    tm, d = x_ref.shape
    gd = d // len(POOL_WINDOWS)
    tile_in_seq = pl.program_id(0) % tiles_per_seq

    @pl.when(tile_in_seq == 0)
    def _():
        p_ref[0:POOL_HALO, :] = jnp.zeros((POOL_HALO, d), F32)

    xb = x_ref[...].astype(BF16)
    pos = tile_in_seq * tm + lax.broadcasted_iota(jnp.int32, (tm, 1), 0)
    for gi, w in enumerate(POOL_WINDOWS):
        cols = slice(gi * gd, (gi + 1) * gd)
        p_ref[POOL_HALO:, cols] = _mm(xb, win_ref[:, cols])
    for gi, w in enumerate(POOL_WINDOWS):
        cols = slice(gi * gd, (gi + 1) * gd)
        ext = p_ref[:, cols]
        p_ref[0:POOL_HALO, cols] = ext[tm:tm + POOL_HALO, :]
        wsum, span = ext, 1
        while span < w:
            wsum = wsum + _shift_rows(wsum, span)
            span *= 2
        inv_count = 1.0 / jnp.minimum(pos + 1, w).astype(F32)
        p = wsum[POOL_HALO:, :] * inv_count - ext[POOL_HALO:, :]
        zb_ref[:, cols] = (_mm(p.astype(BF16), wgrp_ref[gi]) * scale_ref[:, cols]).astype(BF16)

    for rows in _halves(tm):
        y = alpha * x_ref[rows, :] + _mm(zb_ref[rows, :], wout_ref[...])
        o_ref[rows, :] = _layer_norm(y, g_ref[...], b_ref[...])


def _pool_layer(x2, w_in, w_grp, scale, w_out, g, b, j, *, alpha, seq):
    rows, d = x2.shape
    tm = ROW_TILE
    assert seq % tm == 0 and rows % tm == 0
    vec = lambda v: v.reshape(1, d)
    kern = functools.partial(_pool_kernel, alpha=alpha, tiles_per_seq=seq // tm)
    return pl.pallas_call(
        kern,
        out_shape=jax.ShapeDtypeStruct((rows, d), F32),
        grid=(rows // tm,),
        in_specs=[_row_spec(tm, d), _layer_spec(w_in, j), _layer_spec(w_grp, j), _const_spec((1, d)),
                  _layer_spec(w_out, j), _const_spec((1, d)), _const_spec((1, d))],
        out_specs=_row_spec(tm, d),
        scratch_shapes=[pltpu.VMEM((POOL_HALO + tm, d), F32), pltpu.VMEM((tm, d), BF16)],
        compiler_params=_compiler_params(),
        name="pool_layer",
    )(x2, w_in, w_grp, vec(scale), w_out, vec(g), vec(b))


def kernel(x, a_w_in, a_dw, a_dw_b, a_ln_g, a_ln_b, a_w_out, b_w_in, b_ln_g, b_ln_b, b_ws, b_bs, b_w_out, c_w_in, c_w_grp, c_scale, c_w_out, f_w_up, f_dw, f_w_down, ln1_g, ln1_b, ln2_g, ln2_b):
    bsz, seq, d = x.shape
    depth = f_w_up.shape[0]
    alpha = float((2 * depth) ** 0.25)
    n_mixers = 3
    a_w_in, a_w_out, b_w_in, b_w_out, c_w_in, c_w_grp, c_w_out, f_w_up, f_w_down = (
        w.astype(BF16) for w in (a_w_in, a_w_out, b_w_in, b_w_out, c_w_in, c_w_grp, c_w_out, f_w_up, f_w_down))
    h = x.reshape(bsz * seq, d)
    for i in range(depth):
        kind, j = i % n_mixers, i // n_mixers
        if kind == 0:
            h = _conf_layer(h, a_w_in, a_dw, a_dw_b[j], a_ln_g[j], a_ln_b[j], a_w_out,
                            ln1_g[i], ln1_b[i], j, alpha=alpha, seq=seq)
        elif kind == 1:
            h = _sgu_layer(h, b_w_in, b_ln_g[j], b_ln_b[j], b_ws, b_bs[j], b_w_out,
                           ln1_g[i], ln1_b[i], j, alpha=alpha, seq=seq)
        else:
            h = _pool_layer(h, c_w_in, c_w_grp, c_scale[j], c_w_out,
                            ln1_g[i], ln1_b[i], j, alpha=alpha, seq=seq)
        h = _ffn_layer(h, f_w_up, f_dw, f_w_down, ln2_g[i], ln2_b[i], i, alpha=alpha, seq=seq)
    return h.reshape(bsz, seq, d)
```

```python
import functools

import jax
import jax.numpy as jnp
from jax import lax
from jax.experimental import pallas as pl
from jax.experimental.pallas import tpu as pltpu

SUBLANES = 8
LANES = 128
MXU_DIM = 256
VMEM_LIMIT_BYTES = 56 * 1024 * 1024

LN_EPS = 1e-5
CONV_WIDTH = 31
CONV_HALO = 32
CONV_SUB = 4
SGU_CHUNK = 128
POOL_WINDOWS = (2, 4, 8, 16)
POOL_HALO = 16

ROW_TILE = 512
FF_CHUNK = MXU_DIM

F32 = jnp.float32
BF16 = jnp.bfloat16


def _layer_norm(y, g, b):
    mu = jnp.mean(y, axis=-1, keepdims=True)
    d = y - mu
    var = jnp.mean(d * d, axis=-1, keepdims=True)
    return d * lax.rsqrt(var + LN_EPS) * g + b


def _silu(x):
    return x * jax.nn.sigmoid(x)


def _gelu_exact(x):
    return 0.5 * x * (1.0 + lax.erf(x * (2.0 ** -0.5)))


def _shift_rows(e, s):
    return e if s == 0 else pltpu.roll(e, s, 0)


def _mm(a, b):
    return jnp.dot(a, b, preferred_element_type=F32)


def _halves(tm):
    half = tm // 2
    return [slice(0, half), slice(half, tm)]


def _const_spec(shape):
    nd = len(shape)
    return pl.BlockSpec(shape, lambda i: (0,) * nd, pipeline_mode=pl.Buffered(1))


def _layer_spec(stack, j):
    nd = stack.ndim - 1
    return pl.BlockSpec((None,) + stack.shape[1:], lambda i: (j,) + (0,) * nd, pipeline_mode=pl.Buffered(1))


def _row_spec(tm, d):
    return pl.BlockSpec((tm, d), lambda i: (i, 0))


def _compiler_params():
    return pltpu.CompilerParams(dimension_semantics=("arbitrary",), vmem_limit_bytes=VMEM_LIMIT_BYTES)


def _ffn_kernel(x_ref, wup_ref, dw_ref, wdn_ref, g_ref, b_ref, o_ref, carry_ref, act_ref, *,
                alpha, tiles_per_seq, fc):
    tm = x_ref.shape[0]
    d_ff = wdn_ref.shape[0]

    @pl.when(pl.program_id(0) % tiles_per_seq == 0)
    def _():
        carry_ref[...] = jnp.zeros_like(carry_ref)

    xb = x_ref[...].astype(BF16)

    def conv_cols(col):
        cols = slice(col, col + fc)
        h = _mm(xb, wup_ref[:, cols])
        ext = jnp.concatenate([carry_ref[:, cols], h], axis=0)
        carry_ref[:, cols] = h[tm - SUBLANES:, :]
        out = h * dw_ref[2:3, cols]
        out = out + _shift_rows(ext, 1)[SUBLANES:, :] * dw_ref[1:2, cols]
        out = out + _shift_rows(ext, 2)[SUBLANES:, :] * dw_ref[0:1, cols]
        return out

    for col in range(0, d_ff, fc):
        val = conv_cols(d_ff + col)
        act_ref[:, col:col + fc] = (_silu(conv_cols(col)) * val).astype(BF16)
    for rows in _halves(tm):
        y = alpha * x_ref[rows, :] + _mm(act_ref[rows, :], wdn_ref[...])
        o_ref[rows, :] = _layer_norm(y, g_ref[...], b_ref[...])


def _ffn_layer(x2, w_up, dw, w_down, g, b, i, *, alpha, seq):
    rows, d = x2.shape
    d_ff = w_down.shape[1]
    tm, fc = ROW_TILE, FF_CHUNK
    assert d_ff % fc == 0 and seq % tm == 0 and rows % tm == 0
    kern = functools.partial(_ffn_kernel, alpha=alpha, tiles_per_seq=seq // tm, fc=fc)
    return pl.pallas_call(
        kern,
        out_shape=jax.ShapeDtypeStruct((rows, d), F32),
        grid=(rows // tm,),
        in_specs=[_row_spec(tm, d), _layer_spec(w_up, i), _layer_spec(dw, i), _layer_spec(w_down, i),
                  _const_spec((1, d)), _const_spec((1, d))],
        out_specs=_row_spec(tm, d),
        scratch_shapes=[pltpu.VMEM((SUBLANES, 2 * d_ff), F32), pltpu.VMEM((tm, d_ff), BF16)],
        compiler_params=_compiler_params(),
        name="ffn_layer",
    )(x2, w_up, dw, w_down, g.reshape(1, d), b.reshape(1, d))


def _conf_kernel(x_ref, win_ref, dw_ref, dwb_ref, lng_ref, lnb_ref, wout_ref, g_ref, b_ref,
                 o_ref, u_ref, c_ref, hb_ref, *, alpha, tiles_per_seq):
    tm, d = x_ref.shape

    @pl.when(pl.program_id(0) % tiles_per_seq == 0)
    def _():
        u_ref[0:CONV_HALO, :] = jnp.zeros((CONV_HALO, d), F32)

    xb = x_ref[...].astype(BF16)
    for c in range(0, d, MXU_DIM):
        a = _mm(xb, win_ref[:, c:c + MXU_DIM])
        gate = _mm(xb, win_ref[:, d + c:d + c + MXU_DIM])
        u_ref[CONV_HALO:, c:c + MXU_DIM] = a * jax.nn.sigmoid(gate)

    for col in range(0, d, LANES):
        cols = slice(col, col + LANES)
        ext = u_ref[:, cols]
        rolled = [_shift_rows(ext, b) for b in range(CONV_SUB)]
        sums = [None, None]
        for s in range(CONV_WIDTH):
            q, rem = divmod(s, SUBLANES)
            c, b = divmod(rem, CONV_SUB)
            lo = CONV_HALO - SUBLANES * (q + c)
            n = tm + SUBLANES * c
            term = rolled[b][lo:lo + n, :] * dw_ref[CONV_WIDTH - 1 - s:CONV_WIDTH - s, cols]
            sums[c] = term if sums[c] is None else sums[c] + term
        c_ref[:, cols] = sums[0] + _shift_rows(sums[1], CONV_SUB)[SUBLANES:, :]
    u_ref[0:CONV_HALO, :] = u_ref[tm:tm + CONV_HALO, :]

    for rows in _halves(tm):
        h = c_ref[rows, :] + dwb_ref[...]
        hb_ref[rows, :] = _silu(_layer_norm(h, lng_ref[...], lnb_ref[...])).astype(BF16)
        o_ref[rows, :] = alpha * x_ref[rows, :] + _mm(hb_ref[rows, :], wout_ref[...])
    for rows in _halves(tm):
        o_ref[rows, :] = _layer_norm(o_ref[rows, :], g_ref[...], b_ref[...])


def _conf_layer(x2, w_in, dw, dw_b, ln_g, ln_b, w_out, g, b, j, *, alpha, seq):
    rows, d = x2.shape
    tm = ROW_TILE
    assert seq % tm == 0 and rows % tm == 0 and d % MXU_DIM == 0
    vec = lambda v: v.reshape(1, d)
    kern = functools.partial(_conf_kernel, alpha=alpha, tiles_per_seq=seq // tm)
    return pl.pallas_call(
        kern,
        out_shape=jax.ShapeDtypeStruct((rows, d), F32),
        grid=(rows // tm,),
        in_specs=[_row_spec(tm, d), _layer_spec(w_in, j), _layer_spec(dw, j),
                  _const_spec((1, d)), _const_spec((1, d)), _const_spec((1, d)),
                  _layer_spec(w_out, j), _const_spec((1, d)), _const_spec((1, d))],
        out_specs=_row_spec(tm, d),
        scratch_shapes=[pltpu.VMEM((CONV_HALO + tm, d), F32), pltpu.VMEM((tm, d), F32), pltpu.VMEM((tm, d), BF16)],
        compiler_params=_compiler_params(),
        name="conformer_layer",
    )(x2, w_in, dw, vec(dw_b), vec(ln_g), vec(ln_b), w_out, vec(g), vec(b))


def _sgu_kernel(x_ref, win_ref, lng_ref, lnb_ref, ws_ref, bst_ref, wout_ref, g_ref, b_ref,
                o_ref, u_ref, v_ref, vb_ref, us_ref, *, alpha):
    tm, d = x_ref.shape
    heads, chunk, _ = ws_ref.shape
    hd = d // heads
    xb = x_ref[...].astype(BF16)
    col_blocks = [slice(c, c + MXU_DIM) for c in range(0, d, MXU_DIM)]

    for cols in col_blocks:
        v_ref[:, cols] = _gelu_exact(_mm(xb, win_ref[:, slice(d + cols.start, d + cols.stop)]))
    for cols in col_blocks:
        u_ref[:, cols] = _gelu_exact(_mm(xb, win_ref[:, cols]))
    for rows in _halves(tm):
        vb_ref[rows, :] = _layer_norm(v_ref[rows, :], lng_ref[...], lnb_ref[...]).astype(BF16)

    t_idx = lax.broadcasted_iota(jnp.int32, (chunk, chunk), 0)
    s_idx = lax.broadcasted_iota(jnp.int32, (chunk, chunk), 1)
    causal = s_idx <= t_idx
    for h in range(heads):
        wm = jnp.where(causal, ws_ref[h], 0.0).astype(BF16)
        bias = bst_ref[:, h:h + 1]
        cols = slice(h * hd, (h + 1) * hd)
        for n in range(tm // chunk):
            rows = slice(n * chunk, (n + 1) * chunk)
            s = _mm(wm, vb_ref[rows, cols]) + bias
            us_ref[rows, cols] = (u_ref[rows, cols] * s).astype(BF16)

    for rows in _halves(tm):
        y = alpha * x_ref[rows, :] + _mm(us_ref[rows, :], wout_ref[...])
        o_ref[rows, :] = _layer_norm(y, g_ref[...], b_ref[...])


def _sgu_layer(x2, w_in, ln_g, ln_b, ws, bs, w_out, g, b, j, *, alpha, seq):
    rows, d = x2.shape
    tm = ROW_TILE
    assert tm % SGU_CHUNK == 0 and seq % tm == 0 and rows % tm == 0
    bst = jnp.transpose(bs)
    vec = lambda v: v.reshape(1, d)
    kern = functools.partial(_sgu_kernel, alpha=alpha)
    return pl.pallas_call(
        kern,
        out_shape=jax.ShapeDtypeStruct((rows, d), F32),
        grid=(rows // tm,),
        in_specs=[_row_spec(tm, d), _layer_spec(w_in, j), _const_spec((1, d)), _const_spec((1, d)),
                  _layer_spec(ws, j), _const_spec(bst.shape), _layer_spec(w_out, j),
                  _const_spec((1, d)), _const_spec((1, d))],
        out_specs=_row_spec(tm, d),
        scratch_shapes=[pltpu.VMEM((tm, d), F32), pltpu.VMEM((tm, d), F32), pltpu.VMEM((tm, d), BF16),
                        pltpu.VMEM((tm, d), BF16)],
        compiler_params=_compiler_params(),
        name="sgu_layer",
    )(x2, w_in, vec(ln_g), vec(ln_b), ws, bst, w_out, vec(g), vec(b))


def _pool_kernel(x_ref, win_ref, wgrp_ref, scale_ref, wout_ref, g_ref, b_ref,
                 o_ref, p_ref, zb_ref, *, alpha, tiles_per_seq):
    tm, d = x_ref.shape
    gd = d // len(POOL_WINDOWS)
    tile_in_seq = pl.program_id(0) % tiles_per_seq

    @pl.when(tile_in_seq == 0)
    def _():
        p_ref[0:POOL_HALO, :] = jnp.zeros((POOL_HALO, d), F32)

    def project(rows):
        p_ref[POOL_HALO + rows.start:POOL_HALO + rows.stop, :] = _mm(x_ref[rows, :].astype(BF16), win_ref[...])

    def pool_and_out(rows):
        n = rows.stop - rows.start
        pos = tile_in_seq * tm + rows.start + lax.broadcasted_iota(jnp.int32, (n, 1), 0)
        for gi, w in enumerate(POOL_WINDOWS):
            cols = slice(gi * gd, (gi + 1) * gd)
            ext = p_ref[rows.start:rows.stop + POOL_HALO, cols]
            wsum, span = ext, 1
            while span < w:
                wsum = wsum + _shift_rows(wsum, span)
                span *= 2
            inv_count = 1.0 / jnp.minimum(pos + 1, w).astype(F32)
            p = wsum[POOL_HALO:, :] * inv_count - ext[POOL_HALO:, :]
            zb_ref[rows, cols] = (_mm(p.astype(BF16), wgrp_ref[gi]) * scale_ref[:, cols]).astype(BF16)
        y = alpha * x_ref[rows, :] + _mm(zb_ref[rows, :], wout_ref[...])
        o_ref[rows, :] = _layer_norm(y, g_ref[...], b_ref[...])

    first, second = _halves(tm)
    project(first)
    project(second)
    pool_and_out(first)
    pool_and_out(second)
    p_ref[0:POOL_HALO, :] = p_ref[tm:tm + POOL_HALO, :]


def _pool_layer(x2, w_in, w_grp, scale, w_out, g, b, j, *, alpha, seq):
    rows, d = x2.shape
    tm = ROW_TILE
    assert seq % tm == 0 and rows % tm == 0
    vec = lambda v: v.reshape(1, d)
    kern = functools.partial(_pool_kernel, alpha=alpha, tiles_per_seq=seq // tm)
    return pl.pallas_call(
        kern,
        out_shape=jax.ShapeDtypeStruct((rows, d), F32),
        grid=(rows // tm,),
        in_specs=[_row_spec(tm, d), _layer_spec(w_in, j), _layer_spec(w_grp, j), _const_spec((1, d)),
                  _layer_spec(w_out, j), _const_spec((1, d)), _const_spec((1, d))],
        out_specs=_row_spec(tm, d),
        scratch_shapes=[pltpu.VMEM((POOL_HALO + tm, d), F32), pltpu.VMEM((tm, d), BF16)],
        compiler_params=_compiler_params(),
        name="pool_layer",
    )(x2, w_in, w_grp, vec(scale), w_out, vec(g), vec(b))


def kernel(x, a_w_in, a_dw, a_dw_b, a_ln_g, a_ln_b, a_w_out, b_w_in, b_ln_g, b_ln_b, b_ws, b_bs, b_w_out, c_w_in, c_w_grp, c_scale, c_w_out, f_w_up, f_dw, f_w_down, ln1_g, ln1_b, ln2_g, ln2_b):
    bsz, seq, d = x.shape
    depth = f_w_up.shape[0]
    alpha = float((2 * depth) ** 0.25)
    n_mixers = 3
    a_w_in, a_w_out, b_w_in, b_w_out, c_w_in, c_w_grp, c_w_out, f_w_up, f_w_down = (
        w.astype(BF16) for w in (a_w_in, a_w_out, b_w_in, b_w_out, c_w_in, c_w_grp, c_w_out, f_w_up, f_w_down))
    h = x.reshape(bsz * seq, d)
    for i in range(depth):
        kind, j = i % n_mixers, i // n_mixers
        if kind == 0:
            h = _conf_layer(h, a_w_in, a_dw, a_dw_b[j], a_ln_g[j], a_ln_b[j], a_w_out,
                            ln1_g[i], ln1_b[i], j, alpha=alpha, seq=seq)
        elif kind == 1:
            h = _sgu_layer(h, b_w_in, b_ln_g[j], b_ln_b[j], b_ws, b_bs[j], b_w_out,
                           ln1_g[i], ln1_b[i], j, alpha=alpha, seq=seq)
        else:
            h = _pool_layer(h, c_w_in, c_w_grp, c_scale[j], c_w_out,
                            ln1_g[i], ln1_b[i], j, alpha=alpha, seq=seq)
        h = _ffn_layer(h, f_w_up, f_dw, f_w_down, ln2_g[i], ln2_b[i], i, alpha=alpha, seq=seq)
    return h.reshape(bsz, seq, d)
```

```python
import functools

import jax
import jax.numpy as jnp
from jax import lax
from jax.experimental import pallas as pl
from jax.experimental.pallas import tpu as pltpu

SUBLANES = 8
LANES = 128
MXU_DIM = 256
VMEM_LIMIT_BYTES = 56 * 1024 * 1024

LN_EPS = 1e-5
CONV_WIDTH = 31
CONV_HALO = 32
CONV_SUB = 4
SGU_CHUNK = 128
POOL_WINDOWS = (2, 4, 8, 16)
POOL_HALO = 16

ROW_TILE = 512
FF_CHUNK = MXU_DIM

F32 = jnp.float32
BF16 = jnp.bfloat16


def _layer_norm(y, g, b):
    mu = jnp.mean(y, axis=-1, keepdims=True)
    d = y - mu
    var = jnp.mean(d * d, axis=-1, keepdims=True)
    return d * lax.rsqrt(var + LN_EPS) * g + b


def _silu(x):
    return x * jax.nn.sigmoid(x)


def _gelu_exact(x):
    return 0.5 * x * (1.0 + lax.erf(x * (2.0 ** -0.5)))


def _shift_rows(e, s):
    return e if s == 0 else pltpu.roll(e, s, 0)


def _mm(a, b):
    return jnp.dot(a, b, preferred_element_type=F32)


def _halves(tm):
    half = tm // 2
    return [slice(0, half), slice(half, tm)]


def _const_spec(shape):
    nd = len(shape)
    return pl.BlockSpec(shape, lambda i: (0,) * nd, pipeline_mode=pl.Buffered(1))


def _layer_spec(stack, j):
    nd = stack.ndim - 1
    return pl.BlockSpec((None,) + stack.shape[1:], lambda i: (j,) + (0,) * nd, pipeline_mode=pl.Buffered(1))


def _row_spec(tm, d):
    return pl.BlockSpec((tm, d), lambda i: (i, 0))


def _compiler_params():
    return pltpu.CompilerParams(dimension_semantics=("arbitrary",), vmem_limit_bytes=VMEM_LIMIT_BYTES)


def _ffn_kernel(x_ref, wup_ref, dw_ref, wdn_ref, g_ref, b_ref, o_ref, carry_ref, act_ref, *,
                alpha, blocks_per_seq, fc):
    tm = x_ref.shape[0] // 2
    d = x_ref.shape[1]
    d_ff = wdn_ref.shape[0]
    first, second = slice(0, tm), slice(tm, 2 * tm)

    @pl.when(pl.program_id(0) % blocks_per_seq == 0)
    def _():
        carry_ref[...] = jnp.zeros_like(carry_ref)

    def up_chunk(xb, rows, col):
        def conv_cols(c):
            cols = slice(c, c + fc)
            h = _mm(xb, wup_ref[:, cols])
            ext = jnp.concatenate([carry_ref[:, cols], h], axis=0)
            carry_ref[:, cols] = h[tm - SUBLANES:, :]
            out = h * dw_ref[2:3, cols]
            out = out + _shift_rows(ext, 1)[SUBLANES:, :] * dw_ref[1:2, cols]
            out = out + _shift_rows(ext, 2)[SUBLANES:, :] * dw_ref[0:1, cols]
            return out

        val = conv_cols(d_ff + col)
        act_ref[rows, col:col + fc] = (_silu(conv_cols(col)) * val).astype(BF16)

    def down(rows, cols):
        o_ref[rows, cols] = alpha * x_ref[rows, cols] + _mm(act_ref[rows, :], wdn_ref[:, cols])

    def norm(rows):
        o_ref[rows, :] = _layer_norm(o_ref[rows, :], g_ref[...], b_ref[...])

    chunks = list(range(0, d_ff, fc))
    xb = x_ref[first, :].astype(BF16)
    for col in chunks:
        up_chunk(xb, first, col)

    extras = [functools.partial(down, first, slice(c, c + MXU_DIM)) for c in range(0, d, MXU_DIM)]
    extras += [functools.partial(norm, rows) for rows in _halves(tm)]
    xb = x_ref[second, :].astype(BF16)
    for k, col in enumerate(chunks):
        up_chunk(xb, second, col)
        if 1 <= k <= len(extras):
            extras[k - 1]()
    for extra in extras[len(chunks) - 1:]:
        extra()

    for r in _halves(tm):
        rows = slice(tm + r.start, tm + r.stop)
        down(rows, slice(0, d))
        norm(rows)


def _ffn_layer(x2, w_up, dw, w_down, g, b, i, *, alpha, seq):
    rows, d = x2.shape
    d_ff = w_down.shape[1]
    tb, fc = 2 * ROW_TILE, FF_CHUNK
    assert d_ff % fc == 0 and seq % tb == 0 and rows % tb == 0 and d % MXU_DIM == 0
    kern = functools.partial(_ffn_kernel, alpha=alpha, blocks_per_seq=seq // tb, fc=fc)
    return pl.pallas_call(
        kern,
        out_shape=jax.ShapeDtypeStruct((rows, d), F32),
        grid=(rows // tb,),
        in_specs=[_row_spec(tb, d), _layer_spec(w_up, i), _layer_spec(dw, i), _layer_spec(w_down, i),
                  _const_spec((1, d)), _const_spec((1, d))],
        out_specs=_row_spec(tb, d),
        scratch_shapes=[pltpu.VMEM((SUBLANES, 2 * d_ff), F32), pltpu.VMEM((tb, d_ff), BF16)],
        compiler_params=_compiler_params(),
        name="ffn_layer",
    )(x2, w_up, dw, w_down, g.reshape(1, d), b.reshape(1, d))


def _conf_kernel(x_ref, win_ref, dw_ref, dwb_ref, lng_ref, lnb_ref, wout_ref, g_ref, b_ref,
                 o_ref, u_ref, c_ref, hb_ref, *, alpha, tiles_per_seq):
    tm, d = x_ref.shape

    @pl.when(pl.program_id(0) % tiles_per_seq == 0)
    def _():
        u_ref[0:CONV_HALO, :] = jnp.zeros((CONV_HALO, d), F32)

    xb = x_ref[...].astype(BF16)
    for c in range(0, d, MXU_DIM):
        a = _mm(xb, win_ref[:, c:c + MXU_DIM])
        gate = _mm(xb, win_ref[:, d + c:d + c + MXU_DIM])
        u_ref[CONV_HALO:, c:c + MXU_DIM] = a * jax.nn.sigmoid(gate)

    for col in range(0, d, MXU_DIM):
        cols = slice(col, col + MXU_DIM)
        ext = u_ref[:, cols]
        rolled = [_shift_rows(ext, b) for b in range(CONV_SUB)]
        sums = [None, None]
        for s in range(CONV_WIDTH):
            q, rem = divmod(s, SUBLANES)
            c, b = divmod(rem, CONV_SUB)
            lo = CONV_HALO - SUBLANES * (q + c)
            n = tm + SUBLANES * c
            term = rolled[b][lo:lo + n, :] * dw_ref[CONV_WIDTH - 1 - s:CONV_WIDTH - s, cols]
            sums[c] = term if sums[c] is None else sums[c] + term
        c_ref[:, cols] = sums[0] + _shift_rows(sums[1], CONV_SUB)[SUBLANES:, :]
    u_ref[0:CONV_HALO, :] = u_ref[tm:tm + CONV_HALO, :]

    for rows in _halves(tm):
        h = c_ref[rows, :] + dwb_ref[...]
        hb_ref[rows, :] = _silu(_layer_norm(h, lng_ref[...], lnb_ref[...])).astype(BF16)
        o_ref[rows, :] = alpha * x_ref[rows, :] + _mm(hb_ref[rows, :], wout_ref[...])
    for rows in _halves(tm):
        o_ref[rows, :] = _layer_norm(o_ref[rows, :], g_ref[...], b_ref[...])


def _conf_layer(x2, w_in, dw, dw_b, ln_g, ln_b, w_out, g, b, j, *, alpha, seq):
    rows, d = x2.shape
    tm = ROW_TILE
    assert seq % tm == 0 and rows % tm == 0 and d % MXU_DIM == 0
    vec = lambda v: v.reshape(1, d)
    kern = functools.partial(_conf_kernel, alpha=alpha, tiles_per_seq=seq // tm)
    return pl.pallas_call(
        kern,
        out_shape=jax.ShapeDtypeStruct((rows, d), F32),
        grid=(rows // tm,),
        in_specs=[_row_spec(tm, d), _layer_spec(w_in, j), _layer_spec(dw, j),
                  _const_spec((1, d)), _const_spec((1, d)), _const_spec((1, d)),
                  _layer_spec(w_out, j), _const_spec((1, d)), _const_spec((1, d))],
        out_specs=_row_spec(tm, d),
        scratch_shapes=[pltpu.VMEM((CONV_HALO + tm, d), F32), pltpu.VMEM((tm, d), F32), pltpu.VMEM((tm, d), BF16)],
        compiler_params=_compiler_params(),
        name="conformer_layer",
    )(x2, w_in, dw, vec(dw_b), vec(ln_g), vec(ln_b), w_out, vec(g), vec(b))


def _sgu_kernel(x_ref, win_ref, lng_ref, lnb_ref, ws_ref, bst_ref, wout_ref, g_ref, b_ref,
                o_ref, u_ref, v_ref, vb_ref, us_ref, *, alpha):
    tm, d = x_ref.shape
    heads, chunk, _ = ws_ref.shape
    hd = d // heads
    xb = x_ref[...].astype(BF16)
    col_blocks = [slice(c, c + MXU_DIM) for c in range(0, d, MXU_DIM)]

    for cols in col_blocks:
        v_ref[:, cols] = _gelu_exact(_mm(xb, win_ref[:, slice(d + cols.start, d + cols.stop)]))
    for cols in col_blocks:
        u_ref[:, cols] = _gelu_exact(_mm(xb, win_ref[:, cols]))
    for rows in _halves(tm):
        vb_ref[rows, :] = _layer_norm(v_ref[rows, :], lng_ref[...], lnb_ref[...]).astype(BF16)

    t_idx = lax.broadcasted_iota(jnp.int32, (chunk, chunk), 0)
    s_idx = lax.broadcasted_iota(jnp.int32, (chunk, chunk), 1)
    causal = s_idx <= t_idx
    for h in range(heads):
        wm = jnp.where(causal, ws_ref[h], 0.0).astype(BF16)
        bias = bst_ref[:, h:h + 1]
        cols = slice(h * hd, (h + 1) * hd)
        for n in range(tm // chunk):
            rows = slice(n * chunk, (n + 1) * chunk)
            s = _mm(wm, vb_ref[rows, cols]) + bias
            us_ref[rows, cols] = (u_ref[rows, cols] * s).astype(BF16)

    for rows in _halves(tm):
        y = alpha * x_ref[rows, :] + _mm(us_ref[rows, :], wout_ref[...])
        o_ref[rows, :] = _layer_norm(y, g_ref[...], b_ref[...])


def _sgu_layer(x2, w_in, ln_g, ln_b, ws, bs, w_out, g, b, j, *, alpha, seq):
    rows, d = x2.shape
    tm = ROW_TILE
    assert tm % SGU_CHUNK == 0 and seq % tm == 0 and rows % tm == 0
    bst = jnp.transpose(bs)
    vec = lambda v: v.reshape(1, d)
    kern = functools.partial(_sgu_kernel, alpha=alpha)
    return pl.pallas_call(
        kern,
        out_shape=jax.ShapeDtypeStruct((rows, d), F32),
        grid=(rows // tm,),
        in_specs=[_row_spec(tm, d), _layer_spec(w_in, j), _const_spec((1, d)), _const_spec((1, d)),
                  _layer_spec(ws, j), _const_spec(bst.shape), _layer_spec(w_out, j),
                  _const_spec((1, d)), _const_spec((1, d))],
        out_specs=_row_spec(tm, d),
        scratch_shapes=[pltpu.VMEM((tm, d), F32), pltpu.VMEM((tm, d), F32), pltpu.VMEM((tm, d), BF16),
                        pltpu.VMEM((tm, d), BF16)],
        compiler_params=_compiler_params(),
        name="sgu_layer",
    )(x2, w_in, vec(ln_g), vec(ln_b), ws, bst, w_out, vec(g), vec(b))


def _pool_kernel(x_ref, win_ref, wgrp_ref, scale_ref, wout_ref, g_ref, b_ref,
                 o_ref, p_ref, zb_ref, *, alpha, tiles_per_seq):
    tm, d = x_ref.shape
    gd = d // len(POOL_WINDOWS)
    tile_in_seq = pl.program_id(0) % tiles_per_seq

    @pl.when(tile_in_seq == 0)
    def _():
        p_ref[0:POOL_HALO, :] = jnp.zeros((POOL_HALO, d), F32)

    def project(rows):
        p_ref[POOL_HALO + rows.start:POOL_HALO + rows.stop, :] = _mm(x_ref[rows, :].astype(BF16), win_ref[...])

    def pool_and_out(rows):
        n = rows.stop - rows.start
        pos = tile_in_seq * tm + rows.start + lax.broadcasted_iota(jnp.int32, (n, 1), 0)
        for gi, w in enumerate(POOL_WINDOWS):
            cols = slice(gi * gd, (gi + 1) * gd)
            ext = p_ref[rows.start:rows.stop + POOL_HALO, cols]
            wsum, span = ext, 1
            while span < w:
                wsum = wsum + _shift_rows(wsum, span)
                span *= 2
            inv_count = 1.0 / jnp.minimum(pos + 1, w).astype(F32)
            p = wsum[POOL_HALO:, :] * inv_count - ext[POOL_HALO:, :]
            zb_ref[rows, cols] = (_mm(p.astype(BF16), wgrp_ref[gi]) * scale_ref[:, cols]).astype(BF16)
        y = alpha * x_ref[rows, :] + _mm(zb_ref[rows, :], wout_ref[...])
        o_ref[rows, :] = _layer_norm(y, g_ref[...], b_ref[...])

    first, second = _halves(tm)
    project(first)
    project(second)
    pool_and_out(first)
    pool_and_out(second)
    p_ref[0:POOL_HALO, :] = p_ref[tm:tm + POOL_HALO, :]


def _pool_layer(x2, w_in, w_grp, scale, w_out, g, b, j, *, alpha, seq):
    rows, d = x2.shape
    tm = ROW_TILE
    assert seq % tm == 0 and rows % tm == 0
    vec = lambda v: v.reshape(1, d)
    kern = functools.partial(_pool_kernel, alpha=alpha, tiles_per_seq=seq // tm)
    return pl.pallas_call(
        kern,
        out_shape=jax.ShapeDtypeStruct((rows, d), F32),
        grid=(rows // tm,),
        in_specs=[_row_spec(tm, d), _layer_spec(w_in, j), _layer_spec(w_grp, j), _const_spec((1, d)),
                  _layer_spec(w_out, j), _const_spec((1, d)), _const_spec((1, d))],
        out_specs=_row_spec(tm, d),
        scratch_shapes=[pltpu.VMEM((POOL_HALO + tm, d), F32), pltpu.VMEM((tm, d), BF16)],
        compiler_params=_compiler_params(),
        name="pool_layer",
    )(x2, w_in, w_grp, vec(scale), w_out, vec(g), vec(b))


def kernel(x, a_w_in, a_dw, a_dw_b, a_ln_g, a_ln_b, a_w_out, b_w_in, b_ln_g, b_ln_b, b_ws, b_bs, b_w_out, c_w_in, c_w_grp, c_scale, c_w_out, f_w_up, f_dw, f_w_down, ln1_g, ln1_b, ln2_g, ln2_b):
    bsz, seq, d = x.shape
    depth = f_w_up.shape[0]
    alpha = float((2 * depth) ** 0.25)
    n_mixers = 3
    a_w_in, a_w_out, b_w_in, b_w_out, c_w_in, c_w_grp, c_w_out, f_w_up, f_w_down = (
        w.astype(BF16) for w in (a_w_in, a_w_out, b_w_in, b_w_out, c_w_in, c_w_grp, c_w_out, f_w_up, f_w_down))
    h = x.reshape(bsz * seq, d)
    for i in range(depth):
        kind, j = i % n_mixers, i // n_mixers
        if kind == 0:
            h = _conf_layer(h, a_w_in, a_dw, a_dw_b[j], a_ln_g[j], a_ln_b[j], a_w_out,
                            ln1_g[i], ln1_b[i], j, alpha=alpha, seq=seq)
        elif kind == 1:
            h = _sgu_layer(h, b_w_in, b_ln_g[j], b_ln_b[j], b_ws, b_bs[j], b_w_out,
                           ln1_g[i], ln1_b[i], j, alpha=alpha, seq=seq)
        else:
            h = _pool_layer(h, c_w_in, c_w_grp, c_scale[j], c_w_out,
                            ln1_g[i], ln1_b[i], j, alpha=alpha, seq=seq)
        h = _ffn_layer(h, f_w_up, f_dw, f_w_down, ln2_g[i], ln2_b[i], i, alpha=alpha, seq=seq)
    return h.reshape(bsz, seq, d)
```
